```python
import math
import jax, jax.numpy as jnp
from jax import lax
import numpy as np

D_MODEL = 1024
BATCH = 32
SEQ = 256
DEPTH = 1
DEC_BATCH = 2
DEC_SEQ = 2048
PAST_LEN = 256

GRID_W = 64
EPS = 1e-6
HY_CH = 512
HY_SHORT = 3
HY_EMB = 33
HY_FILT_HID = 64
HY_MIN_DECAY = math.log(1e-2) / 1.5
HY_MAX_DECAY = math.log(1e-2) / 0.3
GLA_HEADS = 4
GLA_DK = 64
GLA_DV = 128
GLA_KEY = GLA_HEADS * GLA_DK
GLA_VAL = GLA_HEADS * GLA_DV
GLA_RANK = 16
GLA_GATE_NORM = 16.0
GLA_CHUNK = 64
MIX_W = HY_CH + GLA_VAL
IN_COLS = 3 * HY_CH + 2 * GLA_KEY + 2 * GLA_VAL + 2 * GLA_RANK
PEER_HEADS = 8
PEER_NKEYS = 128
PEER_EXPERTS = PEER_NKEYS * PEER_NKEYS
PEER_DQ = 256
PEER_TOPK = 16
PEER_BLOCK = 128

kernel_name = "hyena_gla_peer_diffusion_step"


def rms_norm(x, g):
    xf = x.astype(jnp.float32)
    y = xf * lax.rsqrt(jnp.mean(xf * xf, axis=-1, keepdims=True) + EPS)
    return (y * g.astype(jnp.float32)).astype(x.dtype)


def short_conv(u, w, b, row_len):
    B, L, C = u.shape
    ur = u.reshape(B, L // row_len, row_len, C)
    pad = HY_SHORT // 2
    up = jnp.pad(ur, ((0, 0), (0, 0), (pad, pad), (0, 0)))
    y = sum(up[:, :, j:j + row_len] * w[j] for j in range(HY_SHORT)) + b
    return y.reshape(B, L, C)


def hyena_filter(L, w1, b1, w2, b2, w3, freq):
    f32 = jnp.float32
    t = jnp.linspace(0.0, 1.0, L, dtype=f32)[:, None]
    bands = (HY_EMB - 1) // 2
    w = 2.0 * math.pi * jnp.arange(L, dtype=f32)[:, None] / L
    f = jnp.linspace(1e-4, bands - 1, bands, dtype=f32)[None, :]
    z = jnp.concatenate([t, jnp.cos(f * w), -jnp.sin(f * w)], axis=-1)
    fr = freq.astype(f32)
    h = jnp.sin(fr * (z @ w1.astype(f32) + b1.astype(f32)))
    h = jnp.sin(fr * (h @ w2.astype(f32) + b2.astype(f32)))
    h = h @ w3.astype(f32)
    deltas = jnp.abs(jnp.linspace(HY_MIN_DECAY, HY_MAX_DECAY, HY_CH, dtype=f32))
    decay = jnp.exp(-t * deltas[None, :])
    return h[:, :HY_CH] * decay, h[:, HY_CH:] * decay


def hyena_mix(u, row_len, conv_w, conv_b, w1, b1, w2, b2, w3, freq, bias):
    B, L, _ = u.shape
    uc = short_conv(u, conv_w, conv_b, row_len).astype(jnp.float32)
    x0, x1, v = jnp.split(uc, 3, axis=-1)
    z = v * x1
    h_f, h_b = hyena_filter(L, w1, b1, w2, b2, w3, freq)
    k_full = jnp.concatenate([h_f, jnp.zeros((1, HY_CH), jnp.float32), h_b[1:][::-1]], axis=0)
    zf = jnp.fft.rfft(z, n=2 * L, axis=1)
    kf = jnp.fft.rfft(k_full, axis=0)
    y = jnp.fft.irfft(zf * kf[None], n=2 * L, axis=1)[:, :L]
    y = y + z * bias.astype(jnp.float32)
    return (y * x0).astype(u.dtype)


def gla_scan(q, k, v, log_a, s0):
    B, L, H, DK = q.shape
    DV = v.shape[-1]
    n = L // GLA_CHUNK

    def to_chunks(a):
        return a.reshape(B, n, GLA_CHUNK, H, a.shape[-1]).transpose(1, 0, 3, 2, 4)

    tri = jnp.tril(jnp.ones((GLA_CHUNK, GLA_CHUNK), dtype=bool))

    def step(S, inp):
        qc, kc, vc, ac = inp
        b = jnp.cumsum(ac, axis=2)
        inter = jnp.einsum('bhtd,bhde->bhte', qc * jnp.exp(b), S)
        diff = b[:, :, :, None, :] - b[:, :, None, :, :]
        dec = jnp.exp(jnp.where(tri[:, :, None], diff, -jnp.inf))
        att = jnp.einsum('bhtd,bhsd,bhtsd->bhts', qc, kc, dec)
        intra = jnp.einsum('bhts,bhse->bhte', att, vc)
        b_end = b[:, :, -1:, :]
        S_new = jnp.exp(b_end[:, :, 0, :])[..., None] * S + jnp.einsum(
            'bhsd,bhse->bhde', kc * jnp.exp(b_end - b), vc)
        return S_new, inter + intra

    S, o = lax.scan(step, s0, (to_chunks(q), to_chunks(k), to_chunks(v), to_chunks(log_a)))
    return o.transpose(1, 0, 3, 2, 4).reshape(B, L, H, DV), S


def gla_mix(q, k, v, og, lr, s0_f, s0_b, w2_f, b_f, w2_b, b_b, gnorm):
    f32 = jnp.float32
    B, L, _ = q.shape
    qh = q.astype(f32).reshape(B, L, GLA_HEADS, GLA_DK) * (GLA_DK ** -0.5)
    kh = k.astype(f32).reshape(B, L, GLA_HEADS, GLA_DK)
    vh = v.astype(f32).reshape(B, L, GLA_HEADS, GLA_DV)
    lrf = lr.astype(f32)
    la_f = (jax.nn.log_sigmoid(lrf[..., :GLA_RANK] @ w2_f.astype(f32) + b_f.astype(f32))
            / GLA_GATE_NORM).reshape(B, L, GLA_HEADS, GLA_DK)
    la_b = (jax.nn.log_sigmoid(lrf[..., GLA_RANK:] @ w2_b.astype(f32) + b_b.astype(f32))
            / GLA_GATE_NORM).reshape(B, L, GLA_HEADS, GLA_DK)
    o_f, s_f = gla_scan(qh, kh, vh, la_f, s0_f)
    o_b, s_b = gla_scan(qh[:, ::-1], kh[:, ::-1], vh[:, ::-1], la_b[:, ::-1], s0_b)
    o = rms_norm(o_f + o_b[:, ::-1], gnorm)
    o = o.reshape(B, L, GLA_VAL) * jax.nn.silu(og.astype(f32))
    return o.astype(q.dtype), s_f, s_b


def peer_ffn(h, wq, sub_keys, u_tab, v_tab):
    f32 = jnp.float32
    B, L, D = h.shape
    T = B * L
    x = h.reshape(T, D)
    q = (x @ wq).reshape(T, PEER_HEADS, 2, PEER_DQ // 2).astype(f32)
    s = jnp.einsum('thpd,hpkd->thpk', q, sub_keys.astype(f32))
    s_top, i_top = lax.top_k(s, PEER_TOPK)
    cand_s = (s_top[:, :, 0, :, None] + s_top[:, :, 1, None, :]).reshape(T, PEER_HEADS, -1)
    cand_i = (i_top[:, :, 0, :, None] * PEER_NKEYS + i_top[:, :, 1, None, :]).reshape(T, PEER_HEADS, -1)
    best, pos = lax.top_k(cand_s, PEER_TOPK)
    idx = jnp.take_along_axis(cand_i, pos, axis=-1)
    gate = jax.nn.softmax(best, axis=-1)
    nb = T // PEER_BLOCK
    xb = x.reshape(nb, PEER_BLOCK, D)
    ib = idx.reshape(nb, PEER_BLOCK, PEER_HEADS * PEER_TOPK)
    gb = gate.reshape(nb, PEER_BLOCK, PEER_HEADS * PEER_TOPK).astype(h.dtype)

    def block(args):
        xt, it, gt = args
        a = jax.nn.gelu(jnp.einsum('td,tkd->tk', xt, u_tab[it]))
        return jnp.einsum('tk,tkd->td', gt * a, v_tab[it])

    y = lax.map(block, (xb, ib, gb))
    return y.reshape(B, L, D)


def trunk_layer(x, mod, s0_f, s0_b, row_len, lw):
    sh1, sc1, g1, sh2, sc2, g2 = jnp.split(mod[:, None, :].astype(x.dtype), 6, axis=-1)
    h = rms_norm(x, lw['norm_pre1']) * (1 + sc1) + sh1
    proj = h @ lw['w_in']
    o0 = 3 * HY_CH
    cuts = [o0, o0 + GLA_KEY, o0 + 2 * GLA_KEY, o0 + 2 * GLA_KEY + GLA_VAL, o0 + 2 * GLA_KEY + 2 * GLA_VAL]
    hy, q, k, v, og, lr = jnp.split(proj, cuts, axis=-1)
    y_hy = hyena_mix(hy, row_len, lw['hy_conv_w'], lw['hy_conv_b'], lw['hf_w1'], lw['hf_b1'],
                     lw['hf_w2'], lw['hf_b2'], lw['hf_w3'], lw['hf_freq'], lw['hy_bias'])
    y_gla, s_f, s_b = gla_mix(q, k, v, og, lr, s0_f, s0_b, lw['gla_w2_f'], lw['gla_b_f'],
                              lw['gla_w2_b'], lw['gla_b_b'], lw['gla_norm'])
    mix = jnp.concatenate([y_hy, y_gla], axis=-1) @ lw['w_out']
    x = x + g1 * rms_norm(mix, lw['norm_post1'])
    h2 = rms_norm(x, lw['norm_pre2']) * (1 + sc2) + sh2
    ff = peer_ffn(h2, lw['peer_wq'], lw['peer_keys'], lw['peer_u'], lw['peer_v'])
    x = x + g2 * rms_norm(ff, lw['norm_post2'])
    return x, s_f, s_b


def setup_inputs(seed: int = 0) -> dict:
    key = jax.random.key(seed)
    ks = jax.random.split(key, 32)
    nrm = jax.random.normal
    D = D_MODEL
    return {
        'x_prompt': nrm(ks[0], (BATCH, SEQ, D), jnp.float32),
        'x_sample': nrm(ks[1], (DEC_BATCH, DEC_SEQ, D), jnp.float32),
        'state_gla': 0.5 * nrm(ks[2], (DEC_BATCH, DEPTH, 2, GLA_HEADS, GLA_DK, GLA_DV), jnp.float32),
        'c': nrm(ks[3], (DEC_BATCH, D), jnp.float32),
        'c_ctx': nrm(ks[4], (D,), jnp.float32),
        'w_ada': 0.5 * D ** -0.5 * nrm(ks[5], (DEPTH, D, 6 * D), jnp.float32),
        'b_ada': 0.02 * nrm(ks[6], (DEPTH, 6 * D), jnp.float32),
        'norm_pre1': 1.0 + 0.05 * nrm(ks[7], (DEPTH, D), jnp.float32),
        'norm_post1': 1.0 + 0.05 * nrm(ks[8], (DEPTH, D), jnp.float32),
        'norm_pre2': 1.0 + 0.05 * nrm(ks[9], (DEPTH, D), jnp.float32),
        'norm_post2': 1.0 + 0.05 * nrm(ks[10], (DEPTH, D), jnp.float32),
        'w_in': D ** -0.5 * nrm(ks[11], (DEPTH, D, IN_COLS), jnp.float32),
        'hy_conv_w': HY_SHORT ** -0.5 * nrm(ks[12], (DEPTH, HY_SHORT, 3 * HY_CH), jnp.float32),
        'hy_conv_b': 0.02 * nrm(ks[13], (DEPTH, 3 * HY_CH), jnp.float32),
        'hf_w1': HY_EMB ** -0.5 * nrm(ks[14], (DEPTH, HY_EMB, HY_FILT_HID), jnp.float32),
        'hf_b1': 0.1 * nrm(ks[15], (DEPTH, HY_FILT_HID), jnp.float32),
        'hf_w2': HY_FILT_HID ** -0.5 * nrm(ks[16], (DEPTH, HY_FILT_HID, HY_FILT_HID), jnp.float32),
        'hf_b2': 0.1 * nrm(ks[17], (DEPTH, HY_FILT_HID), jnp.float32),
        'hf_w3': 0.05 * nrm(ks[18], (DEPTH, HY_FILT_HID, 2 * HY_CH), jnp.float32),
        'hf_freq': 1.0 + 0.05 * nrm(ks[19], (DEPTH, HY_FILT_HID), jnp.float32),
        'hy_bias': 0.1 * nrm(ks[20], (DEPTH, HY_CH), jnp.float32),
        'gla_w2_f': GLA_RANK ** -0.5 * nrm(ks[21], (DEPTH, GLA_RANK, GLA_KEY), jnp.float32),
        'gla_b_f': 0.1 * nrm(ks[22], (DEPTH, GLA_KEY), jnp.float32),
        'gla_w2_b': GLA_RANK ** -0.5 * nrm(ks[23], (DEPTH, GLA_RANK, GLA_KEY), jnp.float32),
        'gla_b_b': 0.1 * nrm(ks[24], (DEPTH, GLA_KEY), jnp.float32),
        'gla_norm': 1.0 + 0.05 * nrm(ks[25], (DEPTH, GLA_DV), jnp.float32),
        'w_out': MIX_W ** -0.5 * nrm(ks[26], (DEPTH, MIX_W, D), jnp.float32),
        'peer_wq': D ** -0.5 * nrm(ks[27], (DEPTH, D, PEER_HEADS * PEER_DQ), jnp.float32),
        'peer_keys': (PEER_DQ // 2) ** -0.5 * nrm(ks[28], (DEPTH, PEER_HEADS, 2, PEER_NKEYS, PEER_DQ // 2), jnp.float32),
        'peer_u': D ** -0.5 * nrm(ks[29], (DEPTH, PEER_EXPERTS, D), jnp.float32),
        'peer_v': (PEER_HEADS * PEER_TOPK) ** -0.5 * nrm(ks[30], (DEPTH, PEER_EXPERTS, D), jnp.float32),
    }


def reference(x_prompt, x_sample, state_gla, c, c_ctx, w_ada, b_ada, norm_pre1, norm_post1,
              norm_pre2, norm_post2, w_in, hy_conv_w, hy_conv_b, hf_w1, hf_b1, hf_w2, hf_b2,
              hf_w3, hf_freq, hy_bias, gla_w2_f, gla_b_f, gla_w2_b, gla_b_b, gla_norm, w_out,
              peer_wq, peer_keys, peer_u, peer_v):
    xp = x_prompt
    xs = x_sample
    ctx_states = []
    for l in range(DEPTH):
        lw = {
            'norm_pre1': norm_pre1[l], 'norm_post1': norm_post1[l],
            'norm_pre2': norm_pre2[l], 'norm_post2': norm_post2[l],
            'w_in': w_in[l], 'hy_conv_w': hy_conv_w[l], 'hy_conv_b': hy_conv_b[l],
            'hf_w1': hf_w1[l], 'hf_b1': hf_b1[l], 'hf_w2': hf_w2[l], 'hf_b2': hf_b2[l],
            'hf_w3': hf_w3[l], 'hf_freq': hf_freq[l], 'hy_bias': hy_bias[l],
            'gla_w2_f': gla_w2_f[l], 'gla_b_f': gla_b_f[l], 'gla_w2_b': gla_w2_b[l],
            'gla_b_b': gla_b_b[l], 'gla_norm': gla_norm[l], 'w_out': w_out[l],
            'peer_wq': peer_wq[l], 'peer_keys': peer_keys[l], 'peer_u': peer_u[l], 'peer_v': peer_v[l],
        }
        mod_ctx = (jax.nn.silu(c_ctx) @ w_ada[l] + b_ada[l])[None, :]
        mod_lat = jax.nn.silu(c) @ w_ada[l] + b_ada[l]
        z0 = jnp.zeros((xp.shape[0], GLA_HEADS, GLA_DK, GLA_DV), jnp.float32)
        xp, s_f, s_b = trunk_layer(xp, mod_ctx, z0, z0, xp.shape[1], lw)
        ctx_states.append(jnp.stack([s_f, s_b], axis=1))
        st = state_gla[:, l].astype(jnp.float32)
        xs, _, _ = trunk_layer(xs, mod_lat, st[:, 0], st[:, 1], GRID_W, lw)
    new_state_gla = jnp.stack(ctx_states, axis=1).astype(x_prompt.dtype)
    return (xp, xs, new_state_gla)
```

```python
import functools
import math

import numpy as np
import jax
import jax.numpy as jnp
from jax import lax
from jax.experimental import pallas as pl
from jax.experimental.pallas import tpu as pltpu

F32 = jnp.float32
BF16 = jnp.bfloat16

D_MODEL = 1024
GRID_W = 64
EPS = 1e-6
HY_CH = 512
HY_SHORT = 3
HY_EMB = 33
HY_FILT_HID = 64
HY_MIN_DECAY = math.log(1e-2) / 1.5
HY_MAX_DECAY = math.log(1e-2) / 0.3
GLA_HEADS = 4
GLA_DK = 64
GLA_DV = 128
GLA_KEY = GLA_HEADS * GLA_DK
GLA_VAL = GLA_HEADS * GLA_DV
GLA_RANK = 16
GLA_GATE_NORM = 16.0
GLA_CHUNK = 64
IN_COLS = 3 * HY_CH + 2 * GLA_KEY + 2 * GLA_VAL + 2 * GLA_RANK
PEER_HEADS = 8
PEER_NKEYS = 128
PEER_EXPERTS = PEER_NKEYS * PEER_NKEYS
PEER_DQ = 256
PEER_TOPK = 16

LANES = 128
IN_COLS_PAD = 3200
VMEM_LIMIT = 56 * 1024 * 1024

_O_Q = 3 * HY_CH
_O_K = _O_Q + GLA_KEY
_O_V = _O_K + GLA_KEY
_O_G = _O_V + GLA_VAL
_O_LR = _O_G + GLA_VAL


def _cparams(sem):
    return pltpu.CompilerParams(dimension_semantics=sem, vmem_limit_bytes=VMEM_LIMIT)


def _dot(a, b):
    return jnp.dot(a, b, preferred_element_type=F32)


def _dot_nt(a, b):
    return lax.dot_general(a, b, (((1,), (1,)), ((), ())), preferred_element_type=F32)


def _dot_tn(a, b):
    return lax.dot_general(a, b, (((0,), (0,)), ((), ())), preferred_element_type=F32)


def _split(a):
    hi = a.astype(BF16)
    lo = (a - hi.astype(F32)).astype(BF16)
    return hi, lo


def _dot3(a, b):
    ah, al = _split(a)
    bh, bl = _split(b)
    return _dot(ah, bh) + _dot(ah, bl) + _dot(al, bh)


def _rms(x):
    return x * lax.rsqrt(jnp.mean(x * x, axis=-1, keepdims=True) + EPS)


def _mod_kernel(c_ref, w_ref, b_ref, o_ref):
    c = c_ref[...]
    a = c * jax.nn.sigmoid(c)
    o_ref[...] = _dot3(a, w_ref[...]) + b_ref[...]


def _mod(c8, w, b):
    n = w.shape[1]
    tn = 1024
    return pl.pallas_call(
        _mod_kernel,
        grid=(n // tn,),
        in_specs=[pl.BlockSpec((8, D_MODEL), lambda j: (0, 0)),
                  pl.BlockSpec((D_MODEL, tn), lambda j: (0, j)),
                  pl.BlockSpec((1, tn), lambda j: (0, j))],
        out_specs=pl.BlockSpec((8, tn), lambda j: (0, j)),
        out_shape=jax.ShapeDtypeStruct((8, n), F32),
        compiler_params=_cparams(("arbitrary",)),
        name="mod",
    )(c8, w, b)


def _inproj_kernel(x_ref, sh_ref, sc_ref, g_ref, w_ref, o_ref):
    h = _rms(x_ref[0]) * g_ref[...] * (1.0 + sc_ref[0]) + sh_ref[0]
    o_ref[0] = _dot(h.astype(BF16), w_ref[...])


def _inproj(x3, mod3, gnorm, w_bf):
    g, lg, d = x3.shape
    n = w_bf.shape[1]
    tm = 256
    return pl.pallas_call(
        _inproj_kernel,
        grid=(g, lg // tm),
        in_specs=[pl.BlockSpec((1, tm, d), lambda b, i: (b, i, 0)),
                  pl.BlockSpec((1, 1, d), lambda b, i: (b, 0, 0)),
                  pl.BlockSpec((1, 1, d), lambda b, i: (b, 0, 1)),
                  pl.BlockSpec((1, d), lambda b, i: (0, 0)),
                  pl.BlockSpec((d, n), lambda b, i: (0, 0))],
        out_specs=pl.BlockSpec((1, tm, n), lambda b, i: (b, i, 0)),
        out_shape=jax.ShapeDtypeStruct((g, lg, n), F32),
        compiler_params=_cparams(("arbitrary", "arbitrary")),
        name="inproj",
    )(x3, mod3, mod3, gnorm, w_bf)


def _hypre_kernel(u_ref, w_ref, b_ref, z_ref, x0_ref, *, row_len):
    u = u_ref[0]
    tm = u.shape[0]
    pos = lax.broadcasted_iota(jnp.int32, (tm, 1), 0) & (row_len - 1)
    prev = jnp.where(pos == 0, 0.0, pltpu.roll(u, 1, axis=0))
    nxt = jnp.where(pos == row_len - 1, 0.0, pltpu.roll(u, tm - 1, axis=0))
    y = prev * w_ref[0:1, :] + u * w_ref[1:2, :] + nxt * w_ref[2:3, :] + b_ref[...]
    x0_ref[0] = y[:, :HY_CH]
    z_ref[0] = y[:, 2 * HY_CH:] * y[:, HY_CH:2 * HY_CH]


def _hypre(proj3, conv_w, conv_b, row_len):
    g, lg, _ = proj3.shape
    tm = 256
    spec_o = pl.BlockSpec((1, tm, HY_CH), lambda b, i: (b, i, 0))
    shp = jax.ShapeDtypeStruct((g, lg, HY_CH), F32)
    return pl.pallas_call(
        functools.partial(_hypre_kernel, row_len=row_len),
        grid=(g, lg // tm),
        in_specs=[pl.BlockSpec((1, tm, 3 * HY_CH), lambda b, i: (b, i, 0)),
                  pl.BlockSpec((HY_SHORT, 3 * HY_CH), lambda b, i: (0, 0)),
                  pl.BlockSpec((1, 3 * HY_CH), lambda b, i: (0, 0))],
        out_specs=[spec_o, spec_o],
        out_shape=[shp, shp],
        compiler_params=_cparams(("arbitrary", "arbitrary")),
        name="hypre",
    )(proj3, conv_w, conv_b)


def _hyfilt_kernel(z_ref, w1_ref, b1_ref, w2_ref, b2_ref, w3_ref, fr_ref, dec_ref, o_ref):
    fr = fr_ref[...]
    h = jnp.sin(fr * (_dot3(z_ref[...], w1_ref[...]) + b1_ref[...]))
    h = jnp.sin(fr * (_dot3(h, w2_ref[...]) + b2_ref[...]))
    h = _dot3(h, w3_ref[...])
    dec = dec_ref[...]
    hf = h[:, :HY_CH] * dec
    hb = h[:, HY_CH:] * dec
    row = pl.program_id(0) * h.shape[0] + lax.broadcasted_iota(jnp.int32, (h.shape[0], 1), 0)
    hb = jnp.where(row == 0, 0.0, hb)
    o_ref[:, :HY_CH] = hf + hb
    o_ref[:, HY_CH:] = hf - hb


def _hyfilt(L, w1, b1, w2, b2, w3, freq):
    t = np.linspace(0.0, 1.0, L, dtype=np.float32)[:, None]
    bands = (HY_EMB - 1) // 2
    w = (2.0 * math.pi * np.arange(L, dtype=np.float32)[:, None] / L).astype(np.float32)
    f = np.linspace(1e-4, bands - 1, bands, dtype=np.float32)[None, :]
    z = np.concatenate([t, np.cos(f * w), -np.sin(f * w)], axis=-1).astype(np.float32)
    zp = np.zeros((L, LANES), np.float32)
    zp[:, :HY_EMB] = z
    deltas = np.abs(np.linspace(HY_MIN_DECAY, HY_MAX_DECAY, HY_CH, dtype=np.float32))
    decay = np.exp(-t * deltas[None, :]).astype(np.float32)

    hid = HY_FILT_HID
    w1p = jnp.zeros((LANES, LANES), F32).at[:HY_EMB, :hid].set(w1)
    w2p = jnp.zeros((LANES, LANES), F32).at[:hid, :hid].set(w2)
    w3p = jnp.zeros((LANES, 2 * HY_CH), F32).at[:hid].set(w3)
    pad1 = lambda v: jnp.zeros((1, LANES), F32).at[0, :hid].set(v)
    tl = 256
    full = lambda shape: pl.BlockSpec(shape, lambda i: (0, 0))
    return pl.pallas_call(
        _hyfilt_kernel,
        grid=(L // tl,),
        in_specs=[pl.BlockSpec((tl, LANES), lambda i: (i, 0)),
                  full((LANES, LANES)), full((1, LANES)), full((LANES, LANES)), full((1, LANES)),
                  full((LANES, 2 * HY_CH)), full((1, LANES)),
                  pl.BlockSpec((tl, HY_CH), lambda i: (i, 0))],
        out_specs=pl.BlockSpec((tl, 2 * HY_CH), lambda i: (i, 0)),
        out_shape=jax.ShapeDtypeStruct((L, 2 * HY_CH), F32),
        compiler_params=_cparams(("arbitrary",)),
        name="hyfilt",
    )(jnp.asarray(zp), w1p, pad1(b1), w2p, pad1(b2), w3p, pad1(freq), jnp.asarray(decay))


@functools.lru_cache(maxsize=None)
def _dft_tables(L):
    k = np.arange(L, dtype=np.int64)[:, None]
    n = np.arange(L, dtype=np.int64)[None, :]
    ang = (np.pi / L) * ((k * n) % (2 * L)).astype(np.float64)
    fwd_b = -np.sin(ang)
    fwd_b[0, :] = np.where(np.arange(L) % 2 == 0, 1.0, -1.0)
    fwd = np.concatenate([np.cos(ang), fwd_b], axis=0)
    inv_scale = np.full((2 * L, 1), 2.0 / (2 * L))
    inv_scale[0] = inv_scale[L] = 1.0 / (2 * L)
    return fwd.astype(np.float32), inv_scale.astype(np.float32)


def _matmul_kernel(a_ref, b_ref, o_ref):
    o_ref[...] = _dot(a_ref[...], b_ref[...].astype(BF16))


def _filter_spectrum(fwd, hsd):
    m, kk = fwd.shape
    n = hsd.shape[1]
    tm = min(m, 512)
    return pl.pallas_call(
        _matmul_kernel,
        grid=(m // tm,),
        in_specs=[pl.BlockSpec((tm, kk), lambda i: (i, 0)),
                  pl.BlockSpec((kk, n), lambda i: (0, 0))],
        out_specs=pl.BlockSpec((tm, n), lambda i: (i, 0)),
        out_shape=jax.ShapeDtypeStruct((m, n), F32),
        compiler_params=_cparams(("arbitrary",)),
        name="hyspec",
    )(fwd, hsd)


def _hyconv_kernel(z_ref, x0_ref, fa_ref, fb_ref, ia_ref, ib_ref, p_ref, q_ref, r_ref, bias_ref,
                   o_ref, acc_ref, zb_ref):
    j = pl.program_id(2)

    @pl.when(j == 0)
    def _():
        zb_ref[...] = z_ref[0].astype(BF16)
        acc_ref[...] = jnp.zeros_like(acc_ref)

    zb = zb_ref[...]
    za = _dot(fa_ref[...], zb)
    zi = _dot(fb_ref[...], zb)
    p = p_ref[...]
    q = q_ref[...]
    ya = za * p - zi * q
    yb = za * q + zi * r_ref[...]
    acc_ref[...] += _dot(ia_ref[...], ya.astype(BF16)) + _dot(ib_ref[...], yb.astype(BF16))

    @pl.when(j == pl.num_programs(2) - 1)
    def _():
        o_ref[0] = (acc_ref[...] + z_ref[0] * bias_ref[...]) * x0_ref[0]


def _hyconv(z, x0, fwd, inv, pk, qk, rk, bias):
    b, L, c = z.shape
    cb = 256
    tk = min(L, 512)
    nk = L // tk
    sig = pl.BlockSpec((1, L, cb), lambda bi, ci, j: (bi, 0, ci))
    coef = pl.BlockSpec((tk, cb), lambda bi, ci, j: (j, ci))
    return pl.pallas_call(
        _hyconv_kernel,
        grid=(b, c // cb, nk),
        in_specs=[sig, sig,
                  pl.BlockSpec((tk, L), lambda bi, ci, j: (j, 0)),
                  pl.BlockSpec((tk, L), lambda bi, ci, j: (nk + j, 0)),
                  pl.BlockSpec((L, tk), lambda bi, ci, j: (0, j)),
                  pl.BlockSpec((L, tk), lambda bi, ci, j: (0, nk + j)),
                  coef, coef, coef,
                  pl.BlockSpec((1, cb), lambda bi, ci, j: (0, ci))],
        out_specs=sig,
        out_shape=jax.ShapeDtypeStruct((b, L, c), F32),
        scratch_shapes=[pltpu.VMEM((L, cb), F32), pltpu.VMEM((L, cb), BF16)],
        compiler_params=_cparams(("arbitrary", "arbitrary", "arbitrary")),
        name="hyconv",
    )(z, x0, fwd, fwd, inv, inv, pk, qk, rk, bias)


def _hyena(proj3, batch, L, row_len, conv_w, conv_b, w1, b1, w2, b2, w3, freq, bias):
    z, x0 = _hypre(proj3, conv_w, conv_b, row_len)
    z = z.reshape(batch, L, HY_CH)
    x0 = x0.reshape(batch, L, HY_CH)
    fwd_np, scale_np = _dft_tables(L)
    fwd = jnp.asarray(fwd_np).astype(BF16)
    inv = (fwd * jnp.asarray(scale_np).astype(BF16)).T
    hsd = _hyfilt(L, w1, b1, w2, b2, w3, freq)
    spec = _filter_spectrum(fwd, hsd)
    row0 = (jnp.arange(L) == 0)[:, None]
    pk = spec[:L, :HY_CH]
    qk = jnp.where(row0, 0.0, spec[L:, HY_CH:])
    rk = jnp.where(row0, spec[L:L + 1, :HY_CH], pk)
    return _hyconv(z, x0, fwd, inv, pk, qk, rk, bias)


def _log_sigmoid(x):
    return jnp.minimum(x, 0.0) - jnp.log(1.0 + jnp.exp(-jnp.abs(x)))


def _gla_chunk(q, k, v, la, st, tri, forward):
    C = q.shape[0]
    la_h, la_l = _split(la)
    trib = jnp.where(tri, 1.0, 0.0).astype(BF16)
    b = _dot(trib, la_h) + _dot(trib, la_l)
    btot = b[C - 1:C, :] if forward else b[0:1, :]
    mid = b[C // 2:C // 2 + 1, :]
    qe = (q * jnp.exp(b - mid)).astype(BF16)
    ke = (k * jnp.exp(mid - b)).astype(BF16)
    qs = (q * jnp.exp(b)).astype(BF16)
    kend = (k * jnp.exp(btot - b)).astype(BF16)
    etot = jnp.exp(btot)
    vb = v.astype(BF16)
    outs, new = [], []
    for h in range(GLA_HEADS):
        ks = slice(h * GLA_DK, (h + 1) * GLA_DK)
        vs = slice(h * GLA_DV, (h + 1) * GLA_DV)
        att = jnp.where(tri, _dot_nt(qe[:, ks], ke[:, ks]), 0.0)
        o = _dot(att.astype(BF16), vb[:, vs]) + _dot_nt(qs[:, ks], st[h].astype(BF16))
        outs.append(o)
        new.append(st[h] * etot[:, ks] + _dot_tn(vb[:, vs], kend[:, ks]))
    return jnp.concatenate(outs, axis=1), tuple(new)


def _gla_kernel(q_ref, k_ref, v_ref, og_ref, lr_ref, s0_ref, wf_ref, bf_ref, wb_ref, bb_ref, gn_ref,
                y_ref, sf_ref, sb_ref, of_ref):
    L = q_ref.shape[1]
    C = GLA_CHUNK
    n = L // C
    ti = lax.broadcasted_iota(jnp.int32, (C, C), 0)
    si = lax.broadcasted_iota(jnp.int32, (C, C), 1)
    tri_f = ti >= si
    tri_b = si >= ti
    scale = GLA_DK ** -0.5

    def load(c):
        rows = pl.ds(pl.multiple_of(c * C, C), C)
        return rows, q_ref[0, rows, :] * scale, k_ref[0, rows, :], v_ref[0, rows, :], lr_ref[0, rows, :]

    def fwd_body(c, st):
        rows, q, k, v, lr = load(c)
        la = _log_sigmoid(_dot3(lr, wf_ref[...]) + bf_ref[...]) * (1.0 / GLA_GATE_NORM)
        o, st = _gla_chunk(q, k, v, la, st, tri_f, True)
        of_ref[rows, :] = o
        return st

    st = lax.fori_loop(0, n, fwd_body, tuple(s0_ref[0, 0, h] for h in range(GLA_HEADS)))
    for h in range(GLA_HEADS):
        sf_ref[0, h] = st[h]

    gn = gn_ref[...]

    def bwd_body(i, st):
        rows, q, k, v, lr = load(n - 1 - i)
        la = _log_sigmoid(_dot3(lr, wb_ref[...]) + bb_ref[...]) * (1.0 / GLA_GATE_NORM)
        o, st = _gla_chunk(q, k, v, la, st, tri_b, False)
        o = o + of_ref[rows, :]
        og = og_ref[0, rows, :]
        gate = og * jax.nn.sigmoid(og)
        for h in range(GLA_HEADS):
            vs = slice(h * GLA_DV, (h + 1) * GLA_DV)
            y_ref[0, rows, vs] = _rms(o[:, vs]) * gn * gate[:, vs]
        return st

    st = lax.fori_loop(0, n, bwd_body, tuple(s0_ref[0, 1, h] for h in range(GLA_HEADS)))
    for h in range(GLA_HEADS):
        sb_ref[0, h] = st[h]


def _gla(proj, s0t, w2f, bf, w2b, bb, gnorm):
    b, L, _ = proj.shape
    w2f_p = jnp.zeros((LANES, GLA_KEY), F32).at[:GLA_RANK].set(w2f)
    w2b_p = jnp.zeros((LANES, GLA_KEY), F32).at[GLA_RANK:2 * GLA_RANK].set(w2b)
    col = lambda width, off: pl.BlockSpec((1, L, width), lambda i: (i, 0, off // width))
    full = lambda shape: pl.BlockSpec(shape, lambda i: (0,) * len(shape))
    st_spec = pl.BlockSpec((1, GLA_HEADS, GLA_DV, GLA_DK), lambda i: (i, 0, 0, 0))
    st_shape = jax.ShapeDtypeStruct((b, GLA_HEADS, GLA_DV, GLA_DK), F32)
    return pl.pallas_call(
        _gla_kernel,
        grid=(b,),
        in_specs=[col(GLA_KEY, _O_Q), col(GLA_KEY, _O_K), col(GLA_VAL, _O_V), col(GLA_VAL, _O_G),
                  col(LANES, _O_LR),
                  pl.BlockSpec((1, 2, GLA_HEADS, GLA_DV, GLA_DK), lambda i: (i, 0, 0, 0, 0)),
                  full((LANES, GLA_KEY)), full((1, GLA_KEY)), full((LANES, GLA_KEY)), full((1, GLA_KEY)),
                  full((1, GLA_DV))],
        out_specs=[pl.BlockSpec((1, L, GLA_VAL), lambda i: (i, 0, 0)), st_spec, st_spec],
        out_shape=[jax.ShapeDtypeStruct((b, L, GLA_VAL), F32), st_shape, st_shape],
        scratch_shapes=[pltpu.VMEM((L, GLA_VAL), F32)],
        compiler_params=_cparams(("arbitrary",)),
        name="gla",
    )(proj, proj, proj, proj, proj, s0t, w2f_p, bf, w2b_p, bb, gnorm)


def _outproj_kernel(yh_ref, yg_ref, x_ref, g1_ref, sh2_ref, sc2_ref, np1_ref, np2_ref, w_ref,
                    x1_ref, h2t_ref):
    mix = _dot(yh_ref[0].astype(BF16), w_ref[:HY_CH, :]) + _dot(yg_ref[0].astype(BF16), w_ref[HY_CH:, :])
    x1 = x_ref[0] + g1_ref[0] * (_rms(mix) * np1_ref[...])
    x1_ref[0] = x1
    h2 = _rms(x1) * np2_ref[...] * (1.0 + sc2_ref[0]) + sh2_ref[0]
    h2t_ref[...] = h2.T.astype(BF16)


def _outproj(yh3, yg3, x3, mod3, npost1, npre2, w_bf):
    g, lg, d = x3.shape
    tm = 256
    nl = lg // tm
    tok = lambda width: pl.BlockSpec((1, tm, width), lambda b, i: (b, i, 0))
    modc = lambda cidx: pl.BlockSpec((1, 1, d), lambda b, i: (b, 0, cidx))
    vec = pl.BlockSpec((1, d), lambda b, i: (0, 0))
    return pl.pallas_call(
        _outproj_kernel,
        grid=(g, nl),
        in_specs=[tok(HY_CH), tok(GLA_VAL), tok(d), modc(2), modc(3), modc(4), vec, vec,
                  pl.BlockSpec((HY_CH + GLA_VAL, d), lambda b, i: (0, 0))],
        out_specs=[tok(d), pl.BlockSpec((d, tm), lambda b, i: (0, b * nl + i))],
        out_shape=[jax.ShapeDtypeStruct((g, lg, d), F32), jax.ShapeDtypeStruct((d, g * lg), BF16)],
        compiler_params=_cparams(("arbitrary", "arbitrary")),
        name="outproj",
    )(yh3, yg3, x3, mod3, mod3, mod3, npost1, npre2, w_bf)


_NEG = -3.0e38


def _top_desc(cur, count):
    vals = []
    for _ in range(count):
        m = jnp.max(cur, axis=0, keepdims=True)
        vals.append(m)
        cur = jnp.where(cur == m, _NEG, cur)
    return vals


_CAND = [(i, j) for i in range(PEER_TOPK) for j in range(PEER_TOPK // (i + 1))]
_CAND_ROWS = -(-len(_CAND) // 8) * 8


def _route_kernel(h_ref, wq_ref, keys_ref, tau_ref, w_ref, s2_ref, e2_ref, qt_ref, cand_ref, bmat_ref):
    qt_ref[...] = _dot(wq_ref[...], h_ref[...])
    tm = h_ref.shape[1]
    half = PEER_DQ // 2
    cand_ref[...] = jnp.full(cand_ref.shape, _NEG, F32)
    for h in range(PEER_HEADS):
        r1 = (2 * h) * half
        s1 = _dot3(keys_ref[2 * h], qt_ref[r1:r1 + half, :])
        s2 = _dot3(keys_ref[2 * h + 1], qt_ref[r1 + half:r1 + 2 * half, :])
        a = _top_desc(s1, PEER_TOPK)
        b = _top_desc(s2, PEER_TOPK)
        for r, (i, j) in enumerate(_CAND):
            cand_ref[r:r + 1, :] = a[i] + b[j]
        best = _top_desc(cand_ref[...], PEER_TOPK)
        m = best[0]
        thr = best[-1]
        zsum = jnp.zeros((1, tm), F32)
        for v in best:
            zsum = zsum + jnp.exp(v - m)
        for jrow in range(PEER_TOPK):
            bmat_ref[jrow:jrow + 1, :] = b[jrow]
        bmat = bmat_ref[...]
        tau = jnp.full(s1.shape, -_NEG, F32)
        for i in range(PEER_TOPK):
            lim = jnp.min(jnp.where(a[i] + bmat >= thr, bmat, -_NEG), axis=0, keepdims=True)
            tau = jnp.where(s1 == a[i], lim, tau)
        tau_ref[h] = tau
        w_ref[h] = jnp.exp(s1 - a[0]) / zsum
        s2_ref[h] = s2
        e2_ref[h] = jnp.exp(s2 - b[0])


def _route(h2t, wqt_bf, keys16):
    d, t = h2t.shape
    tm = 256
    nq = PEER_HEADS * PEER_DQ
    out_spec = pl.BlockSpec((PEER_HEADS, PEER_NKEYS, tm), lambda i: (0, 0, i))
    shp = jax.ShapeDtypeStruct((PEER_HEADS, PEER_NKEYS, t), F32)
    return pl.pallas_call(
        _route_kernel,
        grid=(t // tm,),
        in_specs=[pl.BlockSpec((d, tm), lambda i: (0, i)),
                  pl.BlockSpec((nq, d), lambda i: (0, 0)),
                  pl.BlockSpec((2 * PEER_HEADS, PEER_NKEYS, PEER_DQ // 2), lambda i: (0, 0, 0))],
        out_specs=[out_spec] * 4,
        out_shape=[shp] * 4,
        scratch_shapes=[pltpu.VMEM((nq, tm), F32), pltpu.VMEM((_CAND_ROWS, tm), F32),
                        pltpu.VMEM((PEER_TOPK, tm), F32)],
        compiler_params=_cparams(("arbitrary",)),
        name="route",
    )(h2t, wqt_bf, keys16)


_GELU_C = math.sqrt(2.0 / math.pi)


def _gelu(x):
    return 0.5 * x * (1.0 + jnp.tanh(_GELU_C * (x + 0.044715 * (x * x * x))))


def _experts_kernel(h_ref, u_ref, vt_ref, tau_ref, w_ref, s2_ref, e2_ref, o_ref):
    j = pl.program_id(1)

    @pl.when(j == 0)
    def _():
        o_ref[...] = jnp.zeros_like(o_ref)

    te = u_ref.shape[0]
    nsub = te // PEER_NKEYS
    act = _gelu(_dot(u_ref[...], h_ref[...]))
    parts = []
    for r in range(nsub):
        i1 = j * nsub + r
        g = None
        for h in range(PEER_HEADS):
            tau = tau_ref[h, pl.ds(i1, 1), :]
            wgt = w_ref[h, pl.ds(i1, 1), :]
            term = jnp.where(s2_ref[h] >= tau, e2_ref[h] * wgt, 0.0)
            g = term if g is None else g + term
        parts.append((act[r * PEER_NKEYS:(r + 1) * PEER_NKEYS, :] * g).astype(BF16))
    hid = jnp.concatenate(parts, axis=0)
    o_ref[...] += _dot(vt_ref[...], hid)


def _experts(h2t, u_bf, vt_bf, tau, w, s2, e2):
    d, t = h2t.shape
    tm = 512
    te = 512
    tok = pl.BlockSpec((PEER_HEADS, PEER_NKEYS, tm), lambda i, j: (0, 0, i))
    return pl.pallas_call(
        _experts_kernel,
        grid=(t // tm, PEER_EXPERTS // te),
        in_specs=[pl.BlockSpec((d, tm), lambda i, j: (0, i)),
                  pl.BlockSpec((te, d), lambda i, j: (j, 0)),
                  pl.BlockSpec((d, te), lambda i, j: (0, j)),
                  tok, tok, tok, tok],
        out_specs=pl.BlockSpec((d, tm), lambda i, j: (0, i)),
        out_shape=jax.ShapeDtypeStruct((d, t), F32),
        compiler_params=_cparams(("arbitrary", "arbitrary")),
        name="experts",
    )(h2t, u_bf, vt_bf, tau, w, s2, e2)


def _final_kernel(x1_ref, fft_ref, g2_ref, np_ref, o_ref):
    ff = fft_ref[...].T
    o_ref[0] = x1_ref[0] + g2_ref[0] * (_rms(ff) * np_ref[...])


def _final(x1, fft, mod3, npost2):
    g, lg, d = x1.shape
    tm = 256
    nl = lg // tm
    return pl.pallas_call(
        _final_kernel,
        grid=(g, nl),
        in_specs=[pl.BlockSpec((1, tm, d), lambda b, i: (b, i, 0)),
                  pl.BlockSpec((d, tm), lambda b, i: (0, b * nl + i)),
                  pl.BlockSpec((1, 1, d), lambda b, i: (b, 0, 5)),
                  pl.BlockSpec((1, d), lambda b, i: (0, 0))],
        out_specs=pl.BlockSpec((1, tm, d), lambda b, i: (b, i, 0)),
        out_shape=jax.ShapeDtypeStruct((g, lg, d), F32),
        compiler_params=_cparams(("arbitrary", "arbitrary")),
        name="final",
    )(x1, fft, mod3, npost2)


def _trunk(x3, batch, L, row_len, mod3, s0t, wts):
    proj3 = _inproj(x3, mod3, wts['norm_pre1'], wts['w_in'])
    y_hy = _hyena(proj3, batch, L, row_len, wts['hy_conv_w'], wts['hy_conv_b'], wts['hf_w1'], wts['hf_b1'],
                  wts['hf_w2'], wts['hf_b2'], wts['hf_w3'], wts['hf_freq'], wts['hy_bias'])
    y_gla, sf, sb = _gla(proj3.reshape(batch, L, IN_COLS_PAD), s0t, wts['gla_w2_f'], wts['gla_b_f'],
                         wts['gla_w2_b'], wts['gla_b_b'], wts['gla_norm'])
    g, lg, d = x3.shape
    x1, h2t = _outproj(y_hy.reshape(g, lg, HY_CH), y_gla.reshape(g, lg, GLA_VAL), x3, mod3,
                       wts['norm_post1'], wts['norm_pre2'], wts['w_out'])
    tau, w, s2, e2 = _route(h2t, wts['peer_wqt'], wts['peer_keys'])
    fft = _experts(h2t, wts['peer_u'], wts['peer_vt'], tau, w, s2, e2)
    return _final(x1, fft, mod3, wts['norm_post2']), sf, sb


def kernel(x_prompt, x_sample, state_gla, c, c_ctx, w_ada, b_ada, norm_pre1, norm_post1, norm_pre2, norm_post2, w_in, hy_conv_w, hy_conv_b, hf_w1, hf_b1, hf_w2, hf_b2, hf_w3, hf_freq, hy_bias, gla_w2_f, gla_b_f, gla_w2_b, gla_b_b, gla_norm, w_out, peer_wq, peer_keys, peer_u, peer_v):
    depth = w_ada.shape[0]
    nb, seq, d = x_prompt.shape
    db, dseq, _ = x_sample.shape
    xp = x_prompt.reshape(1, nb * seq, d)
    xs = x_sample
    row = lambda v: v.reshape(1, -1)
    ctx_states = []
    for l in range(depth):
        wts = {
            'norm_pre1': row(norm_pre1[l]), 'norm_post1': row(norm_post1[l]),
            'norm_pre2': row(norm_pre2[l]), 'norm_post2': row(norm_post2[l]),
            'w_in': jnp.pad(w_in[l], ((0, 0), (0, IN_COLS_PAD - IN_COLS))).astype(BF16),
            'hy_conv_w': hy_conv_w[l], 'hy_conv_b': row(hy_conv_b[l]),
            'hf_w1': hf_w1[l], 'hf_b1': hf_b1[l], 'hf_w2': hf_w2[l], 'hf_b2': hf_b2[l],
            'hf_w3': hf_w3[l], 'hf_freq': hf_freq[l], 'hy_bias': row(hy_bias[l]),
            'gla_w2_f': gla_w2_f[l], 'gla_b_f': row(gla_b_f[l]), 'gla_w2_b': gla_w2_b[l],
            'gla_b_b': row(gla_b_b[l]), 'gla_norm': row(gla_norm[l]),
            'w_out': w_out[l].astype(BF16),
            'peer_wqt': peer_wq[l].T.astype(BF16),
            'peer_keys': peer_keys[l].reshape(2 * PEER_HEADS, PEER_NKEYS, PEER_DQ // 2),
            'peer_u': peer_u[l].astype(BF16),
            'peer_vt': peer_v[l].T.astype(BF16),
        }
        c8 = jnp.zeros((8, d), F32).at[0].set(c_ctx).at[1:1 + db].set(c)
        mod = _mod(c8, w_ada[l], row(b_ada[l]))
        mod_ctx = mod[0:1].reshape(1, 1, 6 * d)
        mod_lat = mod[1:1 + db].reshape(db, 1, 6 * d)
        z0 = jnp.zeros((nb, 2, GLA_HEADS, GLA_DV, GLA_DK), F32)
        xp, sf, sb = _trunk(xp, nb, seq, seq, mod_ctx, z0, wts)
        ctx_states.append(jnp.swapaxes(jnp.stack([sf, sb], axis=1), -1, -2))
        st = jnp.swapaxes(state_gla[:, l].astype(F32), -1, -2)
        xs, _, _ = _trunk(xs, db, dseq, GRID_W, mod_lat, st, wts)
    new_state = jnp.stack(ctx_states, axis=1).astype(x_prompt.dtype)
    return xp.reshape(nb, seq, d), xs, new_state
```

```python
import functools
import math

import numpy as np
import jax
import jax.numpy as jnp
from jax import lax
from jax.experimental import pallas as pl
from jax.experimental.pallas import tpu as pltpu

F32 = jnp.float32
BF16 = jnp.bfloat16

D_MODEL = 1024
GRID_W = 64
EPS = 1e-6
HY_CH = 512
HY_SHORT = 3
HY_EMB = 33
HY_FILT_HID = 64
HY_MIN_DECAY = math.log(1e-2) / 1.5
HY_MAX_DECAY = math.log(1e-2) / 0.3
GLA_HEADS = 4
GLA_DK = 64
GLA_DV = 128
GLA_KEY = GLA_HEADS * GLA_DK
GLA_VAL = GLA_HEADS * GLA_DV
GLA_RANK = 16
GLA_GATE_NORM = 16.0
GLA_CHUNK = 64
IN_COLS = 3 * HY_CH + 2 * GLA_KEY + 2 * GLA_VAL + 2 * GLA_RANK
PEER_HEADS = 8
PEER_NKEYS = 128
PEER_EXPERTS = PEER_NKEYS * PEER_NKEYS
PEER_DQ = 256
PEER_TOPK = 16

LANES = 128
IN_COLS_PAD = 3200
VMEM_LIMIT = 56 * 1024 * 1024

_O_Q = 3 * HY_CH
_O_K = _O_Q + GLA_KEY
_O_V = _O_K + GLA_KEY
_O_G = _O_V + GLA_VAL
_O_LR = _O_G + GLA_VAL


def _cparams(sem):
    return pltpu.CompilerParams(dimension_semantics=sem, vmem_limit_bytes=VMEM_LIMIT)


def _dot(a, b):
    return jnp.dot(a, b, preferred_element_type=F32)


def _dot_nt(a, b):
    return lax.dot_general(a, b, (((1,), (1,)), ((), ())), preferred_element_type=F32)


def _dot_tn(a, b):
    return lax.dot_general(a, b, (((0,), (0,)), ((), ())), preferred_element_type=F32)


def _split(a):
    hi = a.astype(BF16)
    lo = (a - hi.astype(F32)).astype(BF16)
    return hi, lo


def _dot3(a, b):
    ah, al = _split(a)
    bh, bl = _split(b)
    return _dot(ah, bh) + _dot(ah, bl) + _dot(al, bh)


def _rms(x):
    return x * lax.rsqrt(jnp.mean(x * x, axis=-1, keepdims=True) + EPS)


def _mod_kernel(c_ref, w_ref, b_ref, o_ref):
    c = c_ref[...]
    a = c * jax.nn.sigmoid(c)
    o_ref[...] = _dot3(a, w_ref[...]) + b_ref[...]


def _mod(c8, w, b):
    n = w.shape[1]
    tn = 1024
    return pl.pallas_call(
        _mod_kernel,
        grid=(n // tn,),
        in_specs=[pl.BlockSpec((8, D_MODEL), lambda j: (0, 0)),
                  pl.BlockSpec((D_MODEL, tn), lambda j: (0, j)),
                  pl.BlockSpec((1, tn), lambda j: (0, j))],
        out_specs=pl.BlockSpec((8, tn), lambda j: (0, j)),
        out_shape=jax.ShapeDtypeStruct((8, n), F32),
        compiler_params=_cparams(("arbitrary",)),
        name="mod",
    )(c8, w, b)


def _inproj_kernel(x_ref, sh_ref, sc_ref, g_ref, w_ref, o_ref):
    h = _rms(x_ref[0]) * g_ref[...] * (1.0 + sc_ref[0]) + sh_ref[0]
    o_ref[0] = _dot(h.astype(BF16), w_ref[...])


def _inproj(x3, mod3, gnorm, w_bf):
    g, lg, d = x3.shape
    n = w_bf.shape[1]
    tm = 256
    return pl.pallas_call(
        _inproj_kernel,
        grid=(g, lg // tm),
        in_specs=[pl.BlockSpec((1, tm, d), lambda b, i: (b, i, 0)),
                  pl.BlockSpec((1, 1, d), lambda b, i: (b, 0, 0)),
                  pl.BlockSpec((1, 1, d), lambda b, i: (b, 0, 1)),
                  pl.BlockSpec((1, d), lambda b, i: (0, 0)),
                  pl.BlockSpec((d, n), lambda b, i: (0, 0))],
        out_specs=pl.BlockSpec((1, tm, n), lambda b, i: (b, i, 0)),
        out_shape=jax.ShapeDtypeStruct((g, lg, n), F32),
        compiler_params=_cparams(("arbitrary", "arbitrary")),
        name="inproj",
    )(x3, mod3, mod3, gnorm, w_bf)


def _hypre_kernel(u_ref, w_ref, b_ref, z_ref, x0_ref, *, row_len):
    u = u_ref[0]
    tm = u.shape[0]
    pos = lax.broadcasted_iota(jnp.int32, (tm, 1), 0) & (row_len - 1)
    prev = jnp.where(pos == 0, 0.0, pltpu.roll(u, 1, axis=0))
    nxt = jnp.where(pos == row_len - 1, 0.0, pltpu.roll(u, tm - 1, axis=0))
    y = prev * w_ref[0:1, :] + u * w_ref[1:2, :] + nxt * w_ref[2:3, :] + b_ref[...]
    x0_ref[0] = y[:, :HY_CH]
    z_ref[0] = y[:, 2 * HY_CH:] * y[:, HY_CH:2 * HY_CH]


def _hypre(proj3, conv_w, conv_b, row_len):
    g, lg, _ = proj3.shape
    tm = 256
    spec_o = pl.BlockSpec((1, tm, HY_CH), lambda b, i: (b, i, 0))
    shp = jax.ShapeDtypeStruct((g, lg, HY_CH), F32)
    return pl.pallas_call(
        functools.partial(_hypre_kernel, row_len=row_len),
        grid=(g, lg // tm),
        in_specs=[pl.BlockSpec((1, tm, 3 * HY_CH), lambda b, i: (b, i, 0)),
                  pl.BlockSpec((HY_SHORT, 3 * HY_CH), lambda b, i: (0, 0)),
                  pl.BlockSpec((1, 3 * HY_CH), lambda b, i: (0, 0))],
        out_specs=[spec_o, spec_o],
        out_shape=[shp, shp],
        compiler_params=_cparams(("arbitrary", "arbitrary")),
        name="hypre",
    )(proj3, conv_w, conv_b)


def _hyfilt_kernel(z_ref, w1_ref, b1_ref, w2_ref, b2_ref, w3_ref, fr_ref, dec_ref, o_ref):
    fr = fr_ref[...]
    h = jnp.sin(fr * (_dot3(z_ref[...], w1_ref[...]) + b1_ref[...]))
    h = jnp.sin(fr * (_dot3(h, w2_ref[...]) + b2_ref[...]))
    h = _dot3(h, w3_ref[...])
    dec = dec_ref[...]
    hf = h[:, :HY_CH] * dec
    hb = h[:, HY_CH:] * dec
    row = pl.program_id(0) * h.shape[0] + lax.broadcasted_iota(jnp.int32, (h.shape[0], 1), 0)
    hb = jnp.where(row == 0, 0.0, hb)
    o_ref[:, :HY_CH] = hf + hb
    o_ref[:, HY_CH:] = hf - hb


def _hyfilt(L, w1, b1, w2, b2, w3, freq):
    t = np.linspace(0.0, 1.0, L, dtype=np.float32)[:, None]
    bands = (HY_EMB - 1) // 2
    w = (2.0 * math.pi * np.arange(L, dtype=np.float32)[:, None] / L).astype(np.float32)
    f = np.linspace(1e-4, bands - 1, bands, dtype=np.float32)[None, :]
    z = np.concatenate([t, np.cos(f * w), -np.sin(f * w)], axis=-1).astype(np.float32)
    zp = np.zeros((L, LANES), np.float32)
    zp[:, :HY_EMB] = z
    deltas = np.abs(np.linspace(HY_MIN_DECAY, HY_MAX_DECAY, HY_CH, dtype=np.float32))
    decay = np.exp(-t * deltas[None, :]).astype(np.float32)

    hid = HY_FILT_HID
    w1p = jnp.zeros((LANES, LANES), F32).at[:HY_EMB, :hid].set(w1)
    w2p = jnp.zeros((LANES, LANES), F32).at[:hid, :hid].set(w2)
    w3p = jnp.zeros((LANES, 2 * HY_CH), F32).at[:hid].set(w3)
    pad1 = lambda v: jnp.zeros((1, LANES), F32).at[0, :hid].set(v)
    tl = 256
    full = lambda shape: pl.BlockSpec(shape, lambda i: (0, 0))
    return pl.pallas_call(
        _hyfilt_kernel,
        grid=(L // tl,),
        in_specs=[pl.BlockSpec((tl, LANES), lambda i: (i, 0)),
                  full((LANES, LANES)), full((1, LANES)), full((LANES, LANES)), full((1, LANES)),
                  full((LANES, 2 * HY_CH)), full((1, LANES)),
                  pl.BlockSpec((tl, HY_CH), lambda i: (i, 0))],
        out_specs=pl.BlockSpec((tl, 2 * HY_CH), lambda i: (i, 0)),
        out_shape=jax.ShapeDtypeStruct((L, 2 * HY_CH), F32),
        compiler_params=_cparams(("arbitrary",)),
        name="hyfilt",
    )(jnp.asarray(zp), w1p, pad1(b1), w2p, pad1(b2), w3p, pad1(freq), jnp.asarray(decay))


@functools.lru_cache(maxsize=None)
def _dft_tables(L):
    k = np.arange(L, dtype=np.int64)[:, None]
    n = np.arange(L, dtype=np.int64)[None, :]
    ang = (np.pi / L) * ((k * n) % (2 * L)).astype(np.float64)
    fwd_b = -np.sin(ang)
    fwd_b[0, :] = np.where(np.arange(L) % 2 == 0, 1.0, -1.0)
    fwd = np.concatenate([np.cos(ang), fwd_b], axis=0)
    inv_scale = np.full((2 * L, 1), 2.0 / (2 * L))
    inv_scale[0] = inv_scale[L] = 1.0 / (2 * L)
    return fwd.astype(np.float32), inv_scale.astype(np.float32)


def _matmul_kernel(a_ref, b_ref, o_ref):
    o_ref[...] = _dot(a_ref[...], b_ref[...].astype(BF16))


def _filter_spectrum(fwd, hsd):
    m, kk = fwd.shape
    n = hsd.shape[1]
    tm = min(m, 512)
    return pl.pallas_call(
        _matmul_kernel,
        grid=(m // tm,),
        in_specs=[pl.BlockSpec((tm, kk), lambda i: (i, 0)),
                  pl.BlockSpec((kk, n), lambda i: (0, 0))],
        out_specs=pl.BlockSpec((tm, n), lambda i: (i, 0)),
        out_shape=jax.ShapeDtypeStruct((m, n), F32),
        compiler_params=_cparams(("arbitrary",)),
        name="hyspec",
    )(fwd, hsd)


def _hyconv_kernel(z_ref, x0_ref, fa_ref, fb_ref, ia_ref, ib_ref, p_ref, q_ref, r_ref, bias_ref,
                   o_ref, acc_ref, zb_ref):
    j = pl.program_id(2)

    @pl.when(j == 0)
    def _():
        zb_ref[...] = z_ref[0].astype(BF16)
        acc_ref[...] = jnp.zeros_like(acc_ref)

    zb = zb_ref[...]
    za = _dot(fa_ref[...], zb)
    zi = _dot(fb_ref[...], zb)
    p = p_ref[...]
    q = q_ref[...]
    ya = za * p - zi * q
    yb = za * q + zi * r_ref[...]
    acc_ref[...] += _dot(ia_ref[...], ya.astype(BF16)) + _dot(ib_ref[...], yb.astype(BF16))

    @pl.when(j == pl.num_programs(2) - 1)
    def _():
        o_ref[0] = (acc_ref[...] + z_ref[0] * bias_ref[...]) * x0_ref[0]


def _hyconv(z, x0, fwd, inv, pk, qk, rk, bias):
    b, L, c = z.shape
    cb = 256
    tk = min(L, 512)
    nk = L // tk
    sig = pl.BlockSpec((1, L, cb), lambda bi, ci, j: (bi, 0, ci))
    coef = pl.BlockSpec((tk, cb), lambda bi, ci, j: (j, ci))
    return pl.pallas_call(
        _hyconv_kernel,
        grid=(b, c // cb, nk),
        in_specs=[sig, sig,
                  pl.BlockSpec((tk, L), lambda bi, ci, j: (j, 0)),
                  pl.BlockSpec((tk, L), lambda bi, ci, j: (nk + j, 0)),
                  pl.BlockSpec((L, tk), lambda bi, ci, j: (0, j)),
                  pl.BlockSpec((L, tk), lambda bi, ci, j: (0, nk + j)),
                  coef, coef, coef,
                  pl.BlockSpec((1, cb), lambda bi, ci, j: (0, ci))],
        out_specs=sig,
        out_shape=jax.ShapeDtypeStruct((b, L, c), F32),
        scratch_shapes=[pltpu.VMEM((L, cb), F32), pltpu.VMEM((L, cb), BF16)],
        compiler_params=_cparams(("arbitrary", "arbitrary", "arbitrary")),
        name="hyconv",
    )(z, x0, fwd, fwd, inv, inv, pk, qk, rk, bias)


def _hyena(proj3, batch, L, row_len, conv_w, conv_b, w1, b1, w2, b2, w3, freq, bias):
    z, x0 = _hypre(proj3, conv_w, conv_b, row_len)
    z = z.reshape(batch, L, HY_CH)
    x0 = x0.reshape(batch, L, HY_CH)
    fwd_np, scale_np = _dft_tables(L)
    fwd = jnp.asarray(fwd_np).astype(BF16)
    inv = (fwd * jnp.asarray(scale_np).astype(BF16)).T
    hsd = _hyfilt(L, w1, b1, w2, b2, w3, freq)
    spec = _filter_spectrum(fwd, hsd)
    row0 = (jnp.arange(L) == 0)[:, None]
    pk = spec[:L, :HY_CH]
    qk = jnp.where(row0, 0.0, spec[L:, HY_CH:])
    rk = jnp.where(row0, spec[L:L + 1, :HY_CH], pk)
    return _hyconv(z, x0, fwd, inv, pk, qk, rk, bias)


def _log_sigmoid(x):
    return jnp.minimum(x, 0.0) - jnp.log(1.0 + jnp.exp(-jnp.abs(x)))


def _gla_chunk(q, k, v, la, st, tri, forward):
    C = q.shape[0]
    la_h, la_l = _split(la)
    trib = jnp.where(tri, 1.0, 0.0).astype(BF16)
    b = _dot(trib, la_h) + _dot(trib, la_l)
    btot = b[C - 1:C, :] if forward else b[0:1, :]
    mid = b[C // 2:C // 2 + 1, :]
    qe = (q * jnp.exp(b - mid)).astype(BF16)
    ke = (k * jnp.exp(mid - b)).astype(BF16)
    qs = (q * jnp.exp(b)).astype(BF16)
    kend = (k * jnp.exp(btot - b)).astype(BF16)
    etot = jnp.exp(btot)
    vb = v.astype(BF16)
    outs, new = [], []
    for h in range(GLA_HEADS):
        ks = slice(h * GLA_DK, (h + 1) * GLA_DK)
        vs = slice(h * GLA_DV, (h + 1) * GLA_DV)
        att = jnp.where(tri, _dot_nt(qe[:, ks], ke[:, ks]), 0.0)
        o = _dot(att.astype(BF16), vb[:, vs]) + _dot_nt(qs[:, ks], st[h].astype(BF16))
        outs.append(o)
        new.append(st[h] * etot[:, ks] + _dot_tn(vb[:, vs], kend[:, ks]))
    return jnp.concatenate(outs, axis=1), tuple(new)


def _gla_kernel(q_ref, k_ref, v_ref, og_ref, lr_ref, s0_ref, wf_ref, bf_ref, wb_ref, bb_ref, gn_ref,
                y_ref, sf_ref, sb_ref, of_ref):
    L = q_ref.shape[1]
    C = GLA_CHUNK
    n = L // C
    ti = lax.broadcasted_iota(jnp.int32, (C, C), 0)
    si = lax.broadcasted_iota(jnp.int32, (C, C), 1)
    tri_f = ti >= si
    tri_b = si >= ti
    scale = GLA_DK ** -0.5

    def load(c):
        rows = pl.ds(pl.multiple_of(c * C, C), C)
        return rows, q_ref[0, rows, :] * scale, k_ref[0, rows, :], v_ref[0, rows, :], lr_ref[0, rows, :]

    def fwd_body(c, st):
        rows, q, k, v, lr = load(c)
        la = _log_sigmoid(_dot3(lr, wf_ref[...]) + bf_ref[...]) * (1.0 / GLA_GATE_NORM)
        o, st = _gla_chunk(q, k, v, la, st, tri_f, True)
        of_ref[rows, :] = o
        return st

    st = lax.fori_loop(0, n, fwd_body, tuple(s0_ref[0, 0, h] for h in range(GLA_HEADS)))
    for h in range(GLA_HEADS):
        sf_ref[0, h] = st[h]

    gn = gn_ref[...]

    def bwd_body(i, st):
        rows, q, k, v, lr = load(n - 1 - i)
        la = _log_sigmoid(_dot3(lr, wb_ref[...]) + bb_ref[...]) * (1.0 / GLA_GATE_NORM)
        o, st = _gla_chunk(q, k, v, la, st, tri_b, False)
        o = o + of_ref[rows, :]
        og = og_ref[0, rows, :]
        gate = og * jax.nn.sigmoid(og)
        for h in range(GLA_HEADS):
            vs = slice(h * GLA_DV, (h + 1) * GLA_DV)
            y_ref[0, rows, vs] = _rms(o[:, vs]) * gn * gate[:, vs]
        return st

    st = lax.fori_loop(0, n, bwd_body, tuple(s0_ref[0, 1, h] for h in range(GLA_HEADS)))
    for h in range(GLA_HEADS):
        sb_ref[0, h] = st[h]


def _gla(proj, s0t, w2f, bf, w2b, bb, gnorm):
    b, L, _ = proj.shape
    w2f_p = jnp.zeros((LANES, GLA_KEY), F32).at[:GLA_RANK].set(w2f)
    w2b_p = jnp.zeros((LANES, GLA_KEY), F32).at[GLA_RANK:2 * GLA_RANK].set(w2b)
    col = lambda width, off: pl.BlockSpec((1, L, width), lambda i: (i, 0, off // width))
    full = lambda shape: pl.BlockSpec(shape, lambda i: (0,) * len(shape))
    st_spec = pl.BlockSpec((1, GLA_HEADS, GLA_DV, GLA_DK), lambda i: (i, 0, 0, 0))
    st_shape = jax.ShapeDtypeStruct((b, GLA_HEADS, GLA_DV, GLA_DK), F32)
    return pl.pallas_call(
        _gla_kernel,
        grid=(b,),
        in_specs=[col(GLA_KEY, _O_Q), col(GLA_KEY, _O_K), col(GLA_VAL, _O_V), col(GLA_VAL, _O_G),
                  col(LANES, _O_LR),
                  pl.BlockSpec((1, 2, GLA_HEADS, GLA_DV, GLA_DK), lambda i: (i, 0, 0, 0, 0)),
                  full((LANES, GLA_KEY)), full((1, GLA_KEY)), full((LANES, GLA_KEY)), full((1, GLA_KEY)),
                  full((1, GLA_DV))],
        out_specs=[pl.BlockSpec((1, L, GLA_VAL), lambda i: (i, 0, 0)), st_spec, st_spec],
        out_shape=[jax.ShapeDtypeStruct((b, L, GLA_VAL), F32), st_shape, st_shape],
        scratch_shapes=[pltpu.VMEM((L, GLA_VAL), F32)],
        compiler_params=_cparams(("arbitrary",)),
        name="gla",
    )(proj, proj, proj, proj, proj, s0t, w2f_p, bf, w2b_p, bb, gnorm)


def _outproj_kernel(yh_ref, yg_ref, x_ref, g1_ref, sh2_ref, sc2_ref, np1_ref, np2_ref, w_ref,
                    x1_ref, h2t_ref):
    mix = _dot(yh_ref[0].astype(BF16), w_ref[:HY_CH, :]) + _dot(yg_ref[0].astype(BF16), w_ref[HY_CH:, :])
    x1 = x_ref[0] + g1_ref[0] * (_rms(mix) * np1_ref[...])
    x1_ref[0] = x1
    h2 = _rms(x1) * np2_ref[...] * (1.0 + sc2_ref[0]) + sh2_ref[0]
    h2t_ref[...] = h2.T.astype(BF16)


def _outproj(yh3, yg3, x3, mod3, npost1, npre2, w_bf):
    g, lg, d = x3.shape
    tm = 256
    nl = lg // tm
    tok = lambda width: pl.BlockSpec((1, tm, width), lambda b, i: (b, i, 0))
    modc = lambda cidx: pl.BlockSpec((1, 1, d), lambda b, i: (b, 0, cidx))
    vec = pl.BlockSpec((1, d), lambda b, i: (0, 0))
    return pl.pallas_call(
        _outproj_kernel,
        grid=(g, nl),
        in_specs=[tok(HY_CH), tok(GLA_VAL), tok(d), modc(2), modc(3), modc(4), vec, vec,
                  pl.BlockSpec((HY_CH + GLA_VAL, d), lambda b, i: (0, 0))],
        out_specs=[tok(d), pl.BlockSpec((d, tm), lambda b, i: (0, b * nl + i))],
        out_shape=[jax.ShapeDtypeStruct((g, lg, d), F32), jax.ShapeDtypeStruct((d, g * lg), BF16)],
        compiler_params=_cparams(("arbitrary", "arbitrary")),
        name="outproj",
    )(yh3, yg3, x3, mod3, mod3, mod3, npost1, npre2, w_bf)


_NEG = -3.0e38


_RANK_OUT = 64.0


def _top_desc(cur, count, with_rank=False):
    vals = []
    rank = jnp.full(cur.shape, _RANK_OUT, F32) if with_rank else None
    for k in range(count):
        m = jnp.max(cur, axis=0, keepdims=True)
        vals.append(m)
        hit = cur == m
        if with_rank:
            rank = jnp.where(hit, float(k + 1), rank)
        cur = jnp.where(hit, _NEG, cur)
    return (vals, rank) if with_rank else vals


_CAND = [(i, j) for i in range(PEER_TOPK) for j in range(PEER_TOPK // (i + 1))]
_CAND_ROWS = -(-len(_CAND) // 8) * 8


def _route_kernel(h_ref, wq_ref, keys_ref, cnt_ref, w_ref, r2_ref, e2_ref, qt_ref, cand_ref, bmat_ref):
    qt_ref[...] = _dot(wq_ref[...], h_ref[...])
    tm = h_ref.shape[1]
    half = PEER_DQ // 2
    cand_ref[...] = jnp.full(cand_ref.shape, _NEG, F32)
    for h in range(PEER_HEADS):
        r1 = (2 * h) * half
        s1 = _dot3(keys_ref[2 * h], qt_ref[r1:r1 + half, :])
        s2 = _dot3(keys_ref[2 * h + 1], qt_ref[r1 + half:r1 + 2 * half, :])
        a = _top_desc(s1, PEER_TOPK)
        b, rank2 = _top_desc(s2, PEER_TOPK, with_rank=True)
        for r, (i, j) in enumerate(_CAND):
            cand_ref[r:r + 1, :] = a[i] + b[j]
        best = _top_desc(cand_ref[...], PEER_TOPK)
        m = best[0]
        thr = best[-1]
        zsum = jnp.zeros((1, tm), F32)
        for v in best:
            zsum = zsum + jnp.exp(v - m)
        for jrow in range(PEER_TOPK):
            bmat_ref[jrow:jrow + 1, :] = b[jrow]
        bmat = bmat_ref[...]
        cnt = jnp.zeros(s1.shape, F32)
        for i in range(PEER_TOPK):
            ci = jnp.sum(jnp.where(a[i] + bmat >= thr, 1.0, 0.0), axis=0, keepdims=True)
            cnt = jnp.where(s1 == a[i], ci, cnt)
        cnt_ref[h] = cnt
        w_ref[h] = 0.5 * jnp.exp(s1 - a[0]) / zsum
        r2_ref[h] = rank2.astype(BF16)
        e2_ref[h] = jnp.exp(s2 - b[0]).astype(BF16)


def _route(h2t, wqt_bf, keys16):
    d, t = h2t.shape
    tm = 256
    nq = PEER_HEADS * PEER_DQ
    out_spec = pl.BlockSpec((PEER_HEADS, PEER_NKEYS, tm), lambda i: (0, 0, i))
    shp = jax.ShapeDtypeStruct((PEER_HEADS, PEER_NKEYS, t), F32)
    shp_b = jax.ShapeDtypeStruct((PEER_HEADS, PEER_NKEYS, t), BF16)
    return pl.pallas_call(
        _route_kernel,
        grid=(t // tm,),
        in_specs=[pl.BlockSpec((d, tm), lambda i: (0, i)),
                  pl.BlockSpec((nq, d), lambda i: (0, 0)),
                  pl.BlockSpec((2 * PEER_HEADS, PEER_NKEYS, PEER_DQ // 2), lambda i: (0, 0, 0))],
        out_specs=[out_spec] * 4,
        out_shape=[shp, shp, shp_b, shp_b],
        scratch_shapes=[pltpu.VMEM((nq, tm), F32), pltpu.VMEM((_CAND_ROWS, tm), F32),
                        pltpu.VMEM((PEER_TOPK, tm), F32)],
        compiler_params=_cparams(("arbitrary",)),
        name="route",
    )(h2t, wqt_bf, keys16)


_GELU_C = math.sqrt(2.0 / math.pi)


_PEER_TE = 512
_PEER_TM = 512


def _experts_kernel(h_ref, u_ref, vt_ref, cnt_ref, w_ref, r2_ref, e2_ref, o_ref, a_ref, t_ref):
    j = pl.program_id(1)

    @pl.when(j == 0)
    def _():
        o_ref[...] = jnp.zeros_like(o_ref)

    nsub = _PEER_TE // PEER_NKEYS
    af = _dot(u_ref[...], h_ref[...])
    a_ref[...] = af.astype(BF16)
    t_ref[...] = jnp.tanh((af * (_GELU_C + (_GELU_C * 0.044715) * (af * af))).astype(BF16))
    for r in range(nsub):
        i1 = j * nsub + r
        rows = slice(r * PEER_NKEYS, (r + 1) * PEER_NKEYS)
        g = None
        for h in range(PEER_HEADS):
            cnt = cnt_ref[h, pl.ds(i1, 1), :].astype(BF16)
            wgt = w_ref[h, pl.ds(i1, 1), :].astype(BF16)
            term = jnp.where(r2_ref[h] <= cnt, e2_ref[h] * wgt, 0.0)
            g = term if g is None else g + term
        a = a_ref[rows, :]
        a_ref[rows, :] = (a + a * t_ref[rows, :]) * g
    o_ref[...] += _dot(vt_ref[...], a_ref[...])


def _experts(h2t, u_bf, vt_bf, cnt, w, r2, e2):
    d, t = h2t.shape
    tm, te = _PEER_TM, _PEER_TE
    tok = pl.BlockSpec((PEER_HEADS, PEER_NKEYS, tm), lambda i, j: (0, 0, i))
    buf = pltpu.VMEM((te, tm), BF16)
    return pl.pallas_call(
        _experts_kernel,
        grid=(t // tm, PEER_EXPERTS // te),
        in_specs=[pl.BlockSpec((d, tm), lambda i, j: (0, i)),
                  pl.BlockSpec((te, d), lambda i, j: (j, 0)),
                  pl.BlockSpec((d, te), lambda i, j: (0, j)),
                  tok, tok, tok, tok],
        out_specs=pl.BlockSpec((d, tm), lambda i, j: (0, i)),
        out_shape=jax.ShapeDtypeStruct((d, t), F32),
        scratch_shapes=[buf, buf],
        compiler_params=_cparams(("arbitrary", "arbitrary")),
        name="experts",
    )(h2t, u_bf, vt_bf, cnt, w, r2, e2)


def _final_kernel(x1_ref, fft_ref, g2_ref, np_ref, o_ref):
    ff = fft_ref[...].T
    o_ref[0] = x1_ref[0] + g2_ref[0] * (_rms(ff) * np_ref[...])


def _final(x1, fft, mod3, npost2):
    g, lg, d = x1.shape
    tm = 256
    nl = lg // tm
    return pl.pallas_call(
        _final_kernel,
        grid=(g, nl),
        in_specs=[pl.BlockSpec((1, tm, d), lambda b, i: (b, i, 0)),
                  pl.BlockSpec((d, tm), lambda b, i: (0, b * nl + i)),
                  pl.BlockSpec((1, 1, d), lambda b, i: (b, 0, 5)),
                  pl.BlockSpec((1, d), lambda b, i: (0, 0))],
        out_specs=pl.BlockSpec((1, tm, d), lambda b, i: (b, i, 0)),
        out_shape=jax.ShapeDtypeStruct((g, lg, d), F32),
        compiler_params=_cparams(("arbitrary", "arbitrary")),
        name="final",
    )(x1, fft, mod3, npost2)


def _trunk(x3, batch, L, row_len, mod3, s0t, wts):
    proj3 = _inproj(x3, mod3, wts['norm_pre1'], wts['w_in'])
    y_hy = _hyena(proj3, batch, L, row_len, wts['hy_conv_w'], wts['hy_conv_b'], wts['hf_w1'], wts['hf_b1'],
                  wts['hf_w2'], wts['hf_b2'], wts['hf_w3'], wts['hf_freq'], wts['hy_bias'])
    y_gla, sf, sb = _gla(proj3.reshape(batch, L, IN_COLS_PAD), s0t, wts['gla_w2_f'], wts['gla_b_f'],
                         wts['gla_w2_b'], wts['gla_b_b'], wts['gla_norm'])
    g, lg, d = x3.shape
    x1, h2t = _outproj(y_hy.reshape(g, lg, HY_CH), y_gla.reshape(g, lg, GLA_VAL), x3, mod3,
                       wts['norm_post1'], wts['norm_pre2'], wts['w_out'])
    tau, w, s2, e2 = _route(h2t, wts['peer_wqt'], wts['peer_keys'])
    fft = _experts(h2t, wts['peer_u'], wts['peer_vt'], tau, w, s2, e2)
    return _final(x1, fft, mod3, wts['norm_post2']), sf, sb


def kernel(x_prompt, x_sample, state_gla, c, c_ctx, w_ada, b_ada, norm_pre1, norm_post1, norm_pre2, norm_post2, w_in, hy_conv_w, hy_conv_b, hf_w1, hf_b1, hf_w2, hf_b2, hf_w3, hf_freq, hy_bias, gla_w2_f, gla_b_f, gla_w2_b, gla_b_b, gla_norm, w_out, peer_wq, peer_keys, peer_u, peer_v):
    depth = w_ada.shape[0]
    nb, seq, d = x_prompt.shape
    db, dseq, _ = x_sample.shape
    xp = x_prompt.reshape(1, nb * seq, d)
    xs = x_sample
    row = lambda v: v.reshape(1, -1)
    ctx_states = []
    for l in range(depth):
        wts = {
            'norm_pre1': row(norm_pre1[l]), 'norm_post1': row(norm_post1[l]),
            'norm_pre2': row(norm_pre2[l]), 'norm_post2': row(norm_post2[l]),
            'w_in': jnp.pad(w_in[l], ((0, 0), (0, IN_COLS_PAD - IN_COLS))).astype(BF16),
            'hy_conv_w': hy_conv_w[l], 'hy_conv_b': row(hy_conv_b[l]),
            'hf_w1': hf_w1[l], 'hf_b1': hf_b1[l], 'hf_w2': hf_w2[l], 'hf_b2': hf_b2[l],
            'hf_w3': hf_w3[l], 'hf_freq': hf_freq[l], 'hy_bias': row(hy_bias[l]),
            'gla_w2_f': gla_w2_f[l], 'gla_b_f': row(gla_b_f[l]), 'gla_w2_b': gla_w2_b[l],
            'gla_b_b': row(gla_b_b[l]), 'gla_norm': row(gla_norm[l]),
            'w_out': w_out[l].astype(BF16),
            'peer_wqt': peer_wq[l].T.astype(BF16),
            'peer_keys': peer_keys[l].reshape(2 * PEER_HEADS, PEER_NKEYS, PEER_DQ // 2),
            'peer_u': peer_u[l].astype(BF16),
            'peer_vt': peer_v[l].T.astype(BF16),
        }
        c8 = jnp.zeros((8, d), F32).at[0].set(c_ctx).at[1:1 + db].set(c)
        mod = _mod(c8, w_ada[l], row(b_ada[l]))
        mod_ctx = mod[0:1].reshape(1, 1, 6 * d)
        mod_lat = mod[1:1 + db].reshape(db, 1, 6 * d)
        z0 = jnp.zeros((nb, 2, GLA_HEADS, GLA_DV, GLA_DK), F32)
        xp, sf, sb = _trunk(xp, nb, seq, seq, mod_ctx, z0, wts)
        ctx_states.append(jnp.swapaxes(jnp.stack([sf, sb], axis=1), -1, -2))
        st = jnp.swapaxes(state_gla[:, l].astype(F32), -1, -2)
        xs, _, _ = _trunk(xs, db, dseq, GRID_W, mod_lat, st, wts)
    new_state = jnp.stack(ctx_states, axis=1).astype(x_prompt.dtype)
    return xp.reshape(nb, seq, d), xs, new_state
```
